```python
import math
import jax
import jax.numpy as jnp
from jax import lax
import numpy as np


D_MODEL = 2048
BATCH = 2
SEQ = 4096
DEPTH = 4

CTX_LEN = 256
GRID_W = 64
N_BRANCH = 3
BRANCH_W = 1024
RNN_BLOCKS = 16
RNN_BLOCK_W = BRANCH_W // RNN_BLOCKS
RNN_CONV_W = 4
RNN_PAD = (2, 1)
LRU_C = 8.0
N_HEADS = 8
HEAD_DIM = 64
V_DIM = 2 * HEAD_DIM
Q_BLOCK = 128
ROPE_BASE = 10000.0
ROPE_PAIRS = HEAD_DIM // 4
SCONV_W = 3
N_EXPERTS = 16
N_GROUPS = 4
EXPERTS_PER_GROUP = N_EXPERTS // N_GROUPS
TOP_K = 2
D_FF = 1024
MOE_BLOCK = 128
EPS = 1e-6

IN_SIZES = (BRANCH_W, BRANCH_W,
            N_HEADS * 2 * HEAD_DIM, N_HEADS * 2 * HEAD_DIM, N_HEADS * V_DIM,
            BRANCH_W, BRANCH_W, BRANCH_W,
            N_BRANCH * D_MODEL)
D_IN = sum(IN_SIZES)

kernel_name = 'hybrid_rglru_diffattn_shortconv_grouped_moe_dit'

F32 = jnp.float32


def rmsnorm(x, g):
    x32 = x.astype(F32)
    y = x32 * lax.rsqrt(jnp.mean(x32 * x32, axis=-1, keepdims=True) + EPS)
    return (y * g.astype(F32)).astype(x.dtype)


def modulate(h, shift, scale):
    return h * (1 + scale) + shift


def dwconv(x, w, pad):
    return lax.conv_general_dilated(x, w[:, None, :].astype(x.dtype), window_strides=(1,),
                                    padding=[pad], dimension_numbers=('NWC', 'WIO', 'NWC'),
                                    feature_group_count=x.shape[-1])


def split_in(u):
    return jnp.split(u, np.cumsum(IN_SIZES)[:-1].tolist(), axis=-1)


def rope2d_tables(pos_row, pos_col):
    inv = ROPE_BASE ** (-jnp.arange(ROPE_PAIRS, dtype=F32) / ROPE_PAIRS)
    ang_r = pos_row[:, None] * inv
    ang_c = pos_col[:, None] * inv
    ang = jnp.concatenate([ang_r, ang_r, ang_c, ang_c], axis=-1)
    return jnp.cos(ang), jnp.sin(ang)


def rope2d(x, cos, sin):
    x32 = x.astype(F32)
    xr = x32.reshape(*x.shape[:-1], 2, 2, ROPE_PAIRS)
    rot = jnp.stack([-xr[..., 1, :], xr[..., 0, :]], axis=-2).reshape(x.shape)
    return (x32 * cos[None, :, None, None, :] + rot * sin[None, :, None, None, :]).astype(x.dtype)


def lru_coeffs(x, w, b, lam):
    bsz, t, _ = x.shape
    x32 = x.astype(F32)
    xb = x32.reshape(bsz, t, RNN_BLOCKS, RNN_BLOCK_W)
    gates = jnp.einsum('btnd,gnde->gbtne', xb, w.astype(F32)).reshape(2, bsz, t, BRANCH_W)
    gates = jax.nn.sigmoid(gates + b.astype(F32)[:, None, None, :])
    r, i = gates[0], gates[1]
    log_a = -LRU_C * r * jax.nn.softplus(-lam.astype(F32))
    a = jnp.exp(log_a)
    return a, jnp.sqrt(-jnp.expm1(2.0 * log_a)) * (i * x32)


def _lin_combine(l, r):
    return (l[0] * r[0], r[0] * l[1] + r[1])


def linear_scan(a, b, h0, reverse):
    if reverse:
        a, b = jnp.flip(a, 1), jnp.flip(b, 1)
    a_cum, h = lax.associative_scan(_lin_combine, (a, b), axis=1)
    h = h + a_cum * h0[:, None, :]
    if reverse:
        h = jnp.flip(h, 1)
        return h, h[:, 0]
    return h, h[:, -1]


def rglru_branch(rx_c, rg_c, rx_l, rg_l, conv_w, conv_b, lru_w, lru_b, lru_lam, ctx_out):
    xc = dwconv(rx_c, conv_w, RNN_PAD) + conv_b
    xl = dwconv(rx_l, conv_w, RNN_PAD) + conv_b
    zeros = jnp.zeros((rx_l.shape[0], BRANCH_W), F32)
    hs_c, hs_l = [], []
    for d, rev in enumerate((False, True)):
        a_c, b_c = lru_coeffs(xc, lru_w[d], lru_b[d], lru_lam[d])
        a_l, b_l = lru_coeffs(xl, lru_w[d], lru_b[d], lru_lam[d])
        h_c, h_ctx_final = linear_scan(a_c, b_c, zeros, rev)
        h_l, _ = linear_scan(a_l, b_l, h_ctx_final, rev)
        hs_c.append(h_c)
        hs_l.append(h_l)
    y_l = ((hs_l[0] + hs_l[1]) * jax.nn.gelu(rg_l.astype(F32))).astype(rx_l.dtype)
    if not ctx_out:
        return None, y_l
    y_c = ((hs_c[0] + hs_c[1]) * jax.nn.gelu(rg_c.astype(F32))).astype(rx_c.dtype)
    return y_c, y_l


def diff_attend(q, k, v, lam):
    s = jnp.einsum('bqhmd,bkhmd->bhmqk', q.astype(F32), k.astype(F32)) * (HEAD_DIM ** -0.5)
    p = jax.nn.softmax(s, axis=-1)
    w = p[:, :, 0] - lam * p[:, :, 1]
    return jnp.einsum('bhqk,bkhe->bqhe', w, v.astype(F32))


def head_norm(o, g, lambda_init):
    y = o * lax.rsqrt(jnp.mean(o * o, axis=-1, keepdims=True) + EPS) * g.astype(F32) * (1.0 - lambda_init)
    return y.reshape(o.shape[0], o.shape[1], N_HEADS * V_DIM)


def diff_attn_branch(q_c, k_c, v_c, q_l, k_l, v_l, cos, sin, lam_p, subln_g, lambda_init, ctx_out):
    bsz, seq, _ = q_l.shape
    ctx_len = q_c.shape[1]
    q_c = q_c.reshape(bsz, ctx_len, N_HEADS, 2, HEAD_DIM)
    k_c = k_c.reshape(bsz, ctx_len, N_HEADS, 2, HEAD_DIM)
    v_c = v_c.reshape(bsz, ctx_len, N_HEADS, V_DIM)
    q_l = rope2d(q_l.reshape(bsz, seq, N_HEADS, 2, HEAD_DIM), cos, sin)
    k_l = rope2d(k_l.reshape(bsz, seq, N_HEADS, 2, HEAD_DIM), cos, sin)
    v_l = v_l.reshape(bsz, seq, N_HEADS, V_DIM)
    lam_p = lam_p.astype(F32)
    lam = jnp.exp(jnp.sum(lam_p[0] * lam_p[1])) - jnp.exp(jnp.sum(lam_p[2] * lam_p[3])) + lambda_init
    k_all = jnp.concatenate([k_l, k_c.astype(k_l.dtype)], axis=1)
    v_all = jnp.concatenate([v_l, v_c.astype(v_l.dtype)], axis=1)
    n_blk = seq // Q_BLOCK
    q_blocks = q_l.reshape(bsz, n_blk, Q_BLOCK, N_HEADS, 2, HEAD_DIM).swapaxes(0, 1)
    o_l = lax.map(lambda qb: diff_attend(qb, k_all, v_all, lam), q_blocks)
    o_l = o_l.swapaxes(0, 1).reshape(bsz, seq, N_HEADS, V_DIM)
    y_l = head_norm(o_l, subln_g, lambda_init).astype(q_l.dtype)
    if not ctx_out:
        return None, y_l
    y_c = head_norm(diff_attend(q_c, k_c, v_c, lam), subln_g, lambda_init).astype(q_c.dtype)
    return y_c, y_l


def short_conv(b_gate, c_gate, xin, w):
    return b_gate * dwconv(c_gate * xin, w, (1, 1))


def merge(y_rnn, y_att, y_conv, gate_logits, w_branch, b_merge, w_out):
    ys = jnp.stack([y_rnn, y_att, y_conv], axis=2)
    proj = jnp.einsum('btnw,nwd->btnd', ys, w_branch)
    g = jax.nn.sigmoid(gate_logits.reshape(*gate_logits.shape[:2], N_BRANCH, D_MODEL) + b_merge)
    return jnp.sum(g * proj, axis=2) @ w_out


def mixer(h_c, h_l, cos, sin, w_in, b_merge, rnn_conv_w, rnn_conv_b, lru_w, lru_b, lru_lam,
          diff_lam, diff_subln_g, sconv_w, w_branch, w_out, lambda_init, ctx_out):
    rx_c, rg_c, q_c, k_c, v_c, bb_c, cc_c, cx_c, gt_c = split_in(h_c @ w_in)
    rx_l, rg_l, q_l, k_l, v_l, bb_l, cc_l, cx_l, gt_l = split_in(h_l @ w_in)
    rnn_c, rnn_l = rglru_branch(rx_c, rg_c, rx_l, rg_l, rnn_conv_w, rnn_conv_b, lru_w, lru_b, lru_lam, ctx_out)
    att_c, att_l = diff_attn_branch(q_c, k_c, v_c, q_l, k_l, v_l, cos, sin, diff_lam, diff_subln_g,
                                    lambda_init, ctx_out)
    out_l = merge(rnn_l, att_l, short_conv(bb_l, cc_l, cx_l, sconv_w), gt_l, w_branch, b_merge, w_out)
    if not ctx_out:
        return None, out_l
    out_c = merge(rnn_c, att_c, short_conv(bb_c, cc_c, cx_c, sconv_w), gt_c, w_branch, b_merge, w_out)
    return out_c, out_l


def moe_ffn(h, router_w, router_b, w_gate, w_up, w_down):
    t_tok, d = h.shape
    n_assign = t_tok * TOP_K
    n_blocks = (n_assign + N_EXPERTS * (MOE_BLOCK - 1) + MOE_BLOCK - 1) // MOE_BLOCK
    aff = jax.nn.sigmoid(jnp.dot(h.astype(F32), router_w.astype(F32)))
    sel = aff + router_b.astype(F32)
    group_score = lax.top_k(sel.reshape(t_tok, N_GROUPS, EXPERTS_PER_GROUP), TOP_K)[0].sum(-1)
    best_group = jnp.argmax(group_score, axis=-1)
    expert_group = jnp.arange(N_EXPERTS) // EXPERTS_PER_GROUP
    sel = jnp.where(expert_group[None, :] == best_group[:, None], sel, -jnp.inf)
    _, e_idx = lax.top_k(sel, TOP_K)
    gate = jnp.take_along_axis(aff, e_idx, axis=1)
    gate = gate / jnp.sum(gate, axis=-1, keepdims=True)
    e_flat = e_idx.reshape(n_assign)
    order = jnp.argsort(e_flat)
    counts = jnp.bincount(e_flat, length=N_EXPERTS)
    padded = (counts + MOE_BLOCK - 1) // MOE_BLOCK * MOE_BLOCK
    starts = jnp.cumsum(counts) - counts
    pad_ends = jnp.cumsum(padded)
    pad_starts = pad_ends - padded
    e_sorted = e_flat[order]
    slot_sorted = (pad_starts[e_sorted] + jnp.arange(n_assign) - starts[e_sorted]).astype(jnp.int32)
    slot = jnp.zeros(n_assign, jnp.int32).at[order].set(slot_sorted)
    row_tok = jnp.full(n_blocks * MOE_BLOCK, t_tok, jnp.int32).at[slot].set(
        jnp.arange(n_assign, dtype=jnp.int32) // TOP_K)
    h_pad = jnp.concatenate([h, jnp.zeros((1, d), h.dtype)], axis=0)
    x_blocks = h_pad[row_tok].reshape(n_blocks, MOE_BLOCK, d)
    block_expert = jnp.minimum(
        jnp.searchsorted(pad_ends, jnp.arange(n_blocks) * MOE_BLOCK, side='right'), N_EXPERTS - 1)

    def expert_block(args):
        xb, e = args
        return (jax.nn.silu(xb @ w_gate[e]) * (xb @ w_up[e])) @ w_down[e]

    y_rows = lax.map(expert_block, (x_blocks, block_expert)).reshape(n_blocks * MOE_BLOCK, d)
    return jnp.einsum('tkd,tk->td', y_rows[slot].reshape(t_tok, TOP_K, d), gate.astype(h.dtype))


def setup_inputs(seed: int = 0) -> dict:
    key = jax.random.key(seed)
    ks = jax.random.split(key, 26)
    D = D_MODEL

    def nrm(k, shape, scale):
        return jax.random.normal(k, shape, F32) * scale

    u = jax.random.uniform(ks[14], (DEPTH, 2, BRANCH_W), F32, 0.9, 0.999)
    s = u ** (1.0 / LRU_C)
    lru_lam = jnp.log(s) - jnp.log1p(-s)
    return {
        'x': nrm(ks[0], (BATCH, SEQ, D), 1.0),
        'c': nrm(ks[1], (BATCH, D), 1.0),
        'ctx': nrm(ks[2], (BATCH, CTX_LEN, D), 1.0),
        'c_ctx': nrm(ks[3], (D,), 1.0),
        'ada_w': nrm(ks[4], (DEPTH, D, 6 * D), 0.5 * D ** -0.5),
        'ada_b': nrm(ks[5], (DEPTH, 6 * D), 0.05),
        'norm1_g': 1.0 + nrm(ks[6], (DEPTH, D), 0.05),
        'norm2_g': 1.0 + nrm(ks[7], (DEPTH, D), 0.05),
        'w_in': nrm(ks[8], (DEPTH, D, D_IN), D ** -0.5),
        'b_merge': nrm(ks[9], (DEPTH, N_BRANCH, D), 0.1),
        'rnn_conv_w': nrm(ks[10], (DEPTH, RNN_CONV_W, BRANCH_W), RNN_CONV_W ** -0.5),
        'rnn_conv_b': nrm(ks[11], (DEPTH, BRANCH_W), 0.05),
        'lru_w': nrm(ks[12], (DEPTH, 2, 2, RNN_BLOCKS, RNN_BLOCK_W, RNN_BLOCK_W), RNN_BLOCK_W ** -0.5),
        'lru_b': nrm(ks[13], (DEPTH, 2, 2, BRANCH_W), 0.1),
        'lru_lam': lru_lam,
        'diff_lam': nrm(ks[15], (DEPTH, 4, HEAD_DIM), 0.1),
        'diff_subln_g': 1.0 + nrm(ks[16], (DEPTH, V_DIM), 0.05),
        'sconv_w': nrm(ks[17], (DEPTH, SCONV_W, BRANCH_W), SCONV_W ** -0.5),
        'w_branch': nrm(ks[18], (DEPTH, N_BRANCH, BRANCH_W, D), BRANCH_W ** -0.5),
        'w_out': nrm(ks[19], (DEPTH, D, D), D ** -0.5),
        'router_w': nrm(ks[20], (D, N_EXPERTS), D ** -0.5),
        'router_b': nrm(ks[21], (N_EXPERTS,), 0.01),
        'exp_w_gate': nrm(ks[22], (DEPTH, N_EXPERTS, D, D_FF), D ** -0.5),
        'exp_w_up': nrm(ks[23], (DEPTH, N_EXPERTS, D, D_FF), D ** -0.5),
        'exp_w_down': nrm(ks[24], (DEPTH, N_EXPERTS, D_FF, D), D_FF ** -0.5),
        'final_g': 1.0 + nrm(ks[25], (D,), 0.05),
    }


def reference(x, c, ctx, c_ctx, ada_w, ada_b, norm1_g, norm2_g, w_in, b_merge, rnn_conv_w, rnn_conv_b,
              lru_w, lru_b, lru_lam, diff_lam, diff_subln_g, sconv_w, w_branch, w_out, router_w, router_b,
              exp_w_gate, exp_w_up, exp_w_down, final_g):
    bsz, seq, d = x.shape
    ctx_len = ctx.shape[1]
    rows = seq // GRID_W
    pos_row = jnp.repeat(jnp.arange(rows), GRID_W).astype(F32)
    pos_col = jnp.tile(jnp.arange(GRID_W), rows).astype(F32)
    cos, sin = rope2d_tables(pos_row, pos_col)
    for i in range(DEPTH):
        last = i == DEPTH - 1
        lambda_init = 0.8 - 0.6 * math.exp(-0.3 * i)
        mod_l = jnp.dot(jax.nn.silu(c), ada_w[i]) + ada_b[i]
        mod_c = jnp.dot(jax.nn.silu(c_ctx), ada_w[i]) + ada_b[i]
        sh1, sc1, g1, sh2, sc2, g2 = jnp.split(mod_l[:, None, :], 6, axis=-1)
        csh1, csc1, cg1, csh2, csc2, cg2 = jnp.split(mod_c, 6)
        h_l = modulate(rmsnorm(x, norm1_g[i]), sh1, sc1)
        h_c = modulate(rmsnorm(ctx, norm1_g[i]), csh1, csc1)
        o_c, o_l = mixer(h_c, h_l, cos, sin, w_in[i], b_merge[i], rnn_conv_w[i], rnn_conv_b[i], lru_w[i],
                         lru_b[i], lru_lam[i], diff_lam[i], diff_subln_g[i], sconv_w[i], w_branch[i], w_out[i],
                         lambda_init, not last)
        x = x + g1 * o_l
        h2_l = modulate(rmsnorm(x, norm2_g[i]), sh2, sc2)
        if last:
            f = moe_ffn(h2_l.reshape(bsz * seq, d), router_w, router_b, exp_w_gate[i], exp_w_up[i], exp_w_down[i])
            x = x + g2 * f.reshape(x.shape)
        else:
            ctx = ctx + cg1 * o_c
            h2_c = modulate(rmsnorm(ctx, norm2_g[i]), csh2, csc2)
            tokens = jnp.concatenate([h2_c.reshape(bsz * ctx_len, d), h2_l.reshape(bsz * seq, d)], axis=0)
            f = moe_ffn(tokens, router_w, router_b, exp_w_gate[i], exp_w_up[i], exp_w_down[i])
            ctx = ctx + cg2 * f[:bsz * ctx_len].reshape(ctx.shape)
            x = x + g2 * f[bsz * ctx_len:].reshape(x.shape)
    return rmsnorm(x, final_g)
```

```python
import functools
import math

import jax
import jax.numpy as jnp
from jax import lax
from jax.experimental import pallas as pl
from jax.experimental.pallas import tpu as pltpu

F32 = jnp.float32
BF16 = jnp.bfloat16

GRID_W = 64
HEAD_DIM = 64
V_DIM = 2 * HEAD_DIM
ROPE_PAIRS = HEAD_DIM // 4
ROPE_BASE = 10000.0
LRU_C = 8.0
RNN_PAD_LEFT = 2
SCONV_PAD_LEFT = 1
N_GROUPS = 4
TOP_K = 2
EPS = 1e-6

LANES = 128
SUBLANES = 8
V7X_VMEM_LIMIT_BYTES = 56 * 1024 * 1024

ROW_TILE = 512
SEQ_TILE = 256
Q_TILE = 256
MOE_ROWS = 256
HIGHEST = lax.Precision.HIGHEST


def _cparams(*sem):
    return pltpu.CompilerParams(dimension_semantics=sem, vmem_limit_bytes=V7X_VMEM_LIMIT_BYTES)


def _sigmoid(x):
    return 1.0 / (1.0 + jnp.exp(-x))


def _silu(x):
    return x * _sigmoid(x)


def _gelu_tanh(x):
    return 0.5 * x * (1.0 + jnp.tanh(math.sqrt(2.0 / math.pi) * (x + 0.044715 * (x * x * x))))


def _dot(a, b):
    return lax.dot_general(a, b, (((1,), (0,)), ((), ())), preferred_element_type=F32)


def _dot_nt(a, b):
    return lax.dot_general(a, b, (((1,), (1,)), ((), ())), preferred_element_type=F32)


def _ada_kernel(c_ref, w_ref, b_ref, o_ref):
    cv = c_ref[...]
    o_ref[0] = jnp.dot(_silu(cv), w_ref[0], precision=HIGHEST, preferred_element_type=F32) + b_ref[0]


def _ada_modulation(cvec, ada_w, ada_b):
    depth, d, six_d = ada_w.shape
    tn = 1536 if six_d % 1536 == 0 else LANES
    return pl.pallas_call(
        _ada_kernel, name="ada_mod",
        grid=(depth, six_d // tn),
        in_specs=[pl.BlockSpec((SUBLANES, d), lambda l, n: (0, 0)),
                  pl.BlockSpec((1, d, tn), lambda l, n: (l, 0, n)),
                  pl.BlockSpec((1, 1, tn), lambda l, n: (l, 0, n))],
        out_specs=pl.BlockSpec((1, SUBLANES, tn), lambda l, n: (l, 0, n)),
        out_shape=jax.ShapeDtypeStruct((depth, SUBLANES, six_d), F32),
        compiler_params=_cparams("parallel", "parallel"),
    )(cvec, ada_w, ada_b.reshape(depth, 1, six_d))


class _Dims:
    def __init__(self, bsz, seq, ctx_len, d):
        self.bsz, self.seq, self.ctx_len, self.d = bsz, seq, ctx_len, d
        self.n_lat = bsz * seq
        self.n_ctx = bsz * ctx_len
        self.n_tok = self.n_lat + self.n_ctx
        assert seq % ROW_TILE == 0 and self.n_ctx % ROW_TILE == 0
        assert seq % SEQ_TILE == 0 and ctx_len % SEQ_TILE == 0
        assert seq % Q_TILE == 0 and ctx_len % Q_TILE == 0 and seq % ctx_len == 0

    def seg(self, m, tile):
        return jnp.where(m < self.n_lat // tile, m // (self.seq // tile), self.bsz)


def _mod_spec(dims, layer, j, tile, grid_rank=1, m_axis=0):
    d = dims.d

    def imap(*ids):
        return (layer, dims.seg(ids[m_axis], tile), 0, j)
    return pl.BlockSpec((None, None, 1, d), imap)


def _rms_mod(x, g, shift, scale):
    y = x * lax.rsqrt(jnp.mean(x * x, axis=-1, keepdims=True) + EPS) * g
    return y * (1.0 + scale) + shift


def _norm_mod_kernel(x_ref, g_ref, sh_ref, sc_ref, h_ref):
    h_ref[...] = _rms_mod(x_ref[...], g_ref[...], sh_ref[...], sc_ref[...]).astype(BF16)


def _norm_mod(dims, x, norm_g, mod4, layer):
    d = dims.d
    tm = ROW_TILE
    return pl.pallas_call(
        _norm_mod_kernel, name="norm_mod",
        grid=(dims.n_tok // tm,),
        in_specs=[pl.BlockSpec((tm, d), lambda m: (m, 0)),
                  pl.BlockSpec((None, 1, d), lambda m: (layer, 0, 0)),
                  _mod_spec(dims, layer, 0, tm),
                  _mod_spec(dims, layer, 1, tm)],
        out_specs=pl.BlockSpec((tm, d), lambda m: (m, 0)),
        out_shape=jax.ShapeDtypeStruct((dims.n_tok, d), BF16),
        compiler_params=_cparams("parallel"),
    )(x, norm_g.reshape(-1, 1, d), mod4, mod4)


def _in_proj_f32_kernel(a_ref, w_ref, o_ref):
    o_ref[...] = _dot(a_ref[...], w_ref[...])


def _in_proj_f32(dims, h, w_in, layer, width, n_lead, n_skip):
    d = dims.d
    tm, tn = ROW_TILE, width
    n_tiles = w_in.shape[2] // tn - n_skip
    return pl.pallas_call(
        _in_proj_f32_kernel, name="in_proj_f32",
        grid=(n_tiles, dims.n_tok // tm),
        in_specs=[pl.BlockSpec((tm, d), lambda n, m: (m, 0)),
                  pl.BlockSpec((None, d, tn), lambda n, m: (layer, 0, jnp.where(n < n_lead, n, n + n_skip)))],
        out_specs=pl.BlockSpec((tm, tn), lambda n, m: (m, n)),
        out_shape=jax.ShapeDtypeStruct((dims.n_tok, n_tiles * tn), F32),
        compiler_params=_cparams("parallel", "parallel"),
    )(h, w_in)


def _rotate_half_pairs(x):
    n = x.shape[-1]
    lane = lax.broadcasted_iota(jnp.int32, x.shape, x.ndim - 1)
    first = (lane % (2 * ROPE_PAIRS)) < ROPE_PAIRS
    return jnp.where(first, -pltpu.roll(x, n - ROPE_PAIRS, x.ndim - 1), pltpu.roll(x, ROPE_PAIRS, x.ndim - 1))


def _in_proj_qkv_kernel(a_ref, w_ref, cos_ref, sin_ref, o_ref, *, reps):
    n = pl.program_id(0)
    acc = _dot(a_ref[...], w_ref[...])

    @pl.when(n < 2)
    def _():
        cos = jnp.concatenate([cos_ref[...]] * reps, axis=1)
        sin = jnp.concatenate([sin_ref[...]] * reps, axis=1)
        scale = jnp.where(n == 0, HEAD_DIM ** -0.5, 1.0).astype(F32)
        o_ref[...] = ((acc * cos + _rotate_half_pairs(acc) * sin) * scale).astype(BF16)

    @pl.when(n == 2)
    def _():
        o_ref[...] = acc.astype(BF16)


def _in_proj_qkv(dims, h, w_in, cos_t, sin_t, layer, width, col_tile0):
    d = dims.d
    tm, tn = ROW_TILE, width
    lat_tiles, per_seq = dims.n_lat // tm, dims.seq // tm

    def tab(n, m):
        return (jnp.where(m < lat_tiles, m % per_seq, per_seq), 0)
    return pl.pallas_call(
        functools.partial(_in_proj_qkv_kernel, reps=tn // LANES), name="in_proj_qkv",
        grid=(3, dims.n_tok // tm),
        in_specs=[pl.BlockSpec((tm, d), lambda n, m: (m, 0)),
                  pl.BlockSpec((None, d, tn), lambda n, m: (layer, 0, col_tile0 + n)),
                  pl.BlockSpec((tm, LANES), tab),
                  pl.BlockSpec((tm, LANES), tab)],
        out_specs=pl.BlockSpec((tm, tn), lambda n, m: (m, n)),
        out_shape=jax.ShapeDtypeStruct((dims.n_tok, 3 * tn), BF16),
        compiler_params=_cparams("parallel", "parallel"),
    )(h, w_in, cos_t, sin_t)


def _seq_flags(dims, m):
    lat_tiles, per_seq = dims.n_lat // SEQ_TILE, dims.seq // SEQ_TILE
    per_ctx = dims.ctx_len // SEQ_TILE
    in_lat = m < lat_tiles
    pos = jnp.where(in_lat, m % per_seq, (m - lat_tiles) % per_ctx)
    n = jnp.where(in_lat, per_seq, per_ctx)
    return pos == 0, pos == n - 1


def _halo_specs(dims, col_tile, width):
    r = SEQ_TILE // SUBLANES
    last = dims.n_tok // SUBLANES - 1
    return [pl.BlockSpec((SEQ_TILE, width), lambda m: (m, col_tile)),
            pl.BlockSpec((SUBLANES, width), lambda m: (jnp.maximum(m * r - 1, 0), col_tile)),
            pl.BlockSpec((SUBLANES, width), lambda m: (jnp.minimum((m + 1) * r, last), col_tile))]


def _dwconv_tile(cur, prev, nxt, first, last, w_ref, pad_left):
    rows = cur.shape[0]
    prev = jnp.where(first, 0.0, prev)
    nxt = jnp.where(last, 0.0, nxt)
    xp = jnp.concatenate([prev, cur, nxt], axis=0)
    out = None
    for k in range(w_ref.shape[0]):
        off = SUBLANES + k - pad_left
        term = w_ref[k:k + 1, :] * xp[off:off + rows, :]
        out = term if out is None else out + term
    return out


def _lru_coef_kernel(cur_ref, prev_ref, next_ref, cw_ref, cb_ref, wp_ref, gb_ref, lam_ref, o_ref, *, dims):
    first, last = _seq_flags(dims, pl.program_id(0))
    xc = _dwconv_tile(cur_ref[...], prev_ref[...], next_ref[...], first, last, cw_ref, RNN_PAD_LEFT) + cb_ref[...]
    lam = lam_ref[...]
    neg_softplus = -(jnp.maximum(-lam, 0.0) + jnp.log1p(jnp.exp(-jnp.abs(lam))))
    for p in range(wp_ref.shape[0]):
        sl = slice(p * LANES, (p + 1) * LANES)
        xg = xc[:, sl]
        g = _dot(xg, wp_ref[p])
        for dr in range(2):
            c0 = 2 * dr * LANES
            r = _sigmoid(g[:, c0:c0 + LANES] + gb_ref[2 * dr:2 * dr + 1, sl])
            i = _sigmoid(g[:, c0 + LANES:c0 + 2 * LANES] + gb_ref[2 * dr + 1:2 * dr + 2, sl])
            log_a = LRU_C * r * neg_softplus[dr:dr + 1, sl]
            o_ref[2 * dr, :, sl] = jnp.exp(log_a)
            th = jnp.tanh(log_a)
            o_ref[2 * dr + 1, :, sl] = jnp.sqrt(-2.0 * th / (1.0 - th)) * (i * xg)


def _lru_coefs(dims, u, conv_w, conv_b, w_pair, gate_b, lam, layer, width):
    n_pair = width // LANES
    return pl.pallas_call(
        functools.partial(_lru_coef_kernel, dims=dims), name="lru_coefs",
        grid=(dims.n_tok // SEQ_TILE,),
        in_specs=_halo_specs(dims, 0, width) + [
            pl.BlockSpec((None,) + conv_w.shape[1:], lambda m: (layer, 0, 0)),
            pl.BlockSpec((None, 1, width), lambda m: (layer, 0, 0)),
            pl.BlockSpec((None, n_pair, LANES, 4 * LANES), lambda m: (layer, 0, 0, 0)),
            pl.BlockSpec((None, 4, width), lambda m: (layer, 0, 0)),
            pl.BlockSpec((None, 2, width), lambda m: (layer, 0, 0))],
        out_specs=pl.BlockSpec((4, SEQ_TILE, width), lambda m: (0, m, 0)),
        out_shape=jax.ShapeDtypeStruct((4, dims.n_tok, width), F32),
        compiler_params=_cparams("parallel"),
    )(u, u, u, conv_w, conv_b.reshape(-1, 1, width), w_pair, gate_b, lam)


def _short_conv_kernel(b_ref, c_ref, cp_ref, cn_ref, x_ref, xp_ref, xn_ref, w_ref, o_ref, *, dims):
    first, last = _seq_flags(dims, pl.program_id(0))
    conv = _dwconv_tile(c_ref[...] * x_ref[...], cp_ref[...] * xp_ref[...], cn_ref[...] * xn_ref[...],
                        first, last, w_ref, SCONV_PAD_LEFT)
    o_ref[...] = (b_ref[...] * conv).astype(BF16)


def _short_conv(dims, u, sconv_w, layer, width, col_tile0):
    cur_b = pl.BlockSpec((SEQ_TILE, width), lambda m: (m, col_tile0))
    return pl.pallas_call(
        functools.partial(_short_conv_kernel, dims=dims), name="short_conv",
        grid=(dims.n_tok // SEQ_TILE,),
        in_specs=[cur_b] + _halo_specs(dims, col_tile0 + 1, width) + _halo_specs(dims, col_tile0 + 2, width) + [
            pl.BlockSpec((None,) + sconv_w.shape[1:], lambda m: (layer, 0, 0))],
        out_specs=pl.BlockSpec((SEQ_TILE, width), lambda m: (m, 0)),
        out_shape=jax.ShapeDtypeStruct((dims.n_tok, width), BF16),
        compiler_params=_cparams("parallel"),
    )(u, u, u, u, u, u, u, sconv_w)


def _scan_chunk(a, b, reverse):
    row = lax.broadcasted_iota(jnp.int32, a.shape, 0)
    for s in (1, 2, 4):
        shift = SUBLANES - s if reverse else s
        keep = (row < SUBLANES - s) if reverse else (row >= s)
        a_sh = pltpu.roll(a, shift, 0)
        b_sh = pltpu.roll(b, shift, 0)
        b = jnp.where(keep, a * b_sh + b, b)
        a = jnp.where(keep, a * a_sh, a)
    return a, b


def _lru_scan_kernel(coef_ref, rg_ref, h0_ref, yin_ref, y_ref, hfin_ref, hf_scr, hr_scr, *, n_chunks):
    del yin_ref

    def body(i, carry):
        cf, cr = carry
        rf = pl.multiple_of(i * SUBLANES, SUBLANES)
        rr = pl.multiple_of((n_chunks - 1 - i) * SUBLANES, SUBLANES)
        af, bf = _scan_chunk(coef_ref[0, pl.ds(rf, SUBLANES), :], coef_ref[1, pl.ds(rf, SUBLANES), :], False)
        ar, br = _scan_chunk(coef_ref[2, pl.ds(rr, SUBLANES), :], coef_ref[3, pl.ds(rr, SUBLANES), :], True)
        hf = af * cf + bf
        hr = ar * cr + br
        hf_scr[pl.ds(rf, SUBLANES), :] = hf
        hr_scr[pl.ds(rr, SUBLANES), :] = hr
        return hf[SUBLANES - 1:SUBLANES, :], hr[0:1, :]

    cf, cr = lax.fori_loop(0, n_chunks, body, (h0_ref[0:1, :], h0_ref[1:2, :]), unroll=4)
    row = lax.broadcasted_iota(jnp.int32, hfin_ref.shape, 0)
    hfin_ref[...] = jnp.where(row == 0, cf, jnp.where(row == 1, cr, 0.0))
    y_ref[...] = ((hf_scr[...] + hr_scr[...]) * _gelu_tanh(rg_ref[...])).astype(BF16)


def _lru_scan(dims, coef, u, h0, y_prev, width, rg_col_tile, ctx_part):
    rows = dims.ctx_len if ctx_part else dims.seq
    row0 = dims.n_lat // rows if ctx_part else 0
    tc = LANES
    n_ct = width // tc
    rg0 = rg_col_tile * n_ct
    y_shape = jax.ShapeDtypeStruct((dims.n_tok, width), BF16)
    fin_shape = jax.ShapeDtypeStruct((dims.bsz, SUBLANES, width), F32)
    in_specs = [pl.BlockSpec((4, rows, tc), lambda b, c: (0, row0 + b, c)),
                pl.BlockSpec((rows, tc), lambda b, c: (row0 + b, rg0 + c)),
                pl.BlockSpec((None, SUBLANES, tc), lambda b, c: (b, 0, c))]
    args = [coef, u, h0]
    aliases = {}
    kern = functools.partial(_lru_scan_kernel, n_chunks=rows // SUBLANES)
    if y_prev is not None:
        in_specs.append(pl.BlockSpec(memory_space=pl.ANY))
        args.append(y_prev)
        aliases = {3: 0}
    else:
        kern = functools.partial(lambda *refs, **kw: _lru_scan_kernel(*refs[:3], None, *refs[3:], **kw),
                                 n_chunks=rows // SUBLANES)
    return pl.pallas_call(
        kern, name="lru_scan_ctx" if ctx_part else "lru_scan_lat",
        grid=(dims.bsz, n_ct),
        in_specs=in_specs,
        out_specs=[pl.BlockSpec((rows, tc), lambda b, c: (row0 + b, c)),
                   pl.BlockSpec((None, SUBLANES, tc), lambda b, c: (b, 0, c))],
        out_shape=[y_shape, fin_shape],
        scratch_shapes=[pltpu.VMEM((rows, tc), F32), pltpu.VMEM((rows, tc), F32)],
        input_output_aliases=aliases,
        compiler_params=_cparams("parallel", "parallel"),
    )(*args)


def _diff_lambda(lam_ref, lambda_init):
    lp = lam_ref[...]
    l1 = jnp.exp(jnp.sum(lp[0:1, :] * lp[1:2, :], axis=-1, keepdims=True))
    l2 = jnp.exp(jnp.sum(lp[2:3, :] * lp[3:4, :], axis=-1, keepdims=True))
    return l1 - l2 + lambda_init


def _diff_attn_kernel(*refs, n_kv, lambda_init):
    q_ref = refs[0]
    k_refs = refs[1:1 + n_kv]
    v_refs = refs[1 + n_kv:1 + 2 * n_kv]
    lam_ref, g_ref = refs[1 + 2 * n_kv:3 + 2 * n_kv]
    o_ref = refs[-1]
    q = q_ref[...]
    tq = q.shape[0]
    lane = lax.broadcasted_iota(jnp.int32, q.shape, 1)
    zero = jnp.zeros_like(q)
    q2 = jnp.concatenate([jnp.where(lane < HEAD_DIM, q, zero), jnp.where(lane >= HEAD_DIM, q, zero)], axis=0)
    s = [_dot_nt(q2, k[...]) for k in k_refs]
    m = functools.reduce(jnp.maximum, [jnp.max(x, axis=-1, keepdims=True) for x in s])
    p = [jnp.exp(x - m) for x in s]
    denom = functools.reduce(jnp.add, [jnp.sum(x, axis=-1, keepdims=True) for x in p])
    inv = 1.0 / denom
    lam = _diff_lambda(lam_ref, lambda_init)
    c1 = inv[:tq]
    c2 = inv[tq:] * lam
    o = None
    for x, v in zip(p, v_refs):
        w = (x[:tq] * c1 - x[tq:] * c2).astype(BF16)
        t = _dot(w, v[...])
        o = t if o is None else o + t
    y = o * lax.rsqrt(jnp.mean(o * o, axis=-1, keepdims=True) + EPS) * g_ref[...] * (1.0 - lambda_init)
    o_ref[...] = y.astype(BF16)


def _diff_attn(dims, qkv, lam_p, subln_g, y_prev, layer, n_heads, lambda_init, ctx_part):
    tq = Q_TILE
    seq, ctx_len = dims.seq, dims.ctx_len
    ctx_blk0 = dims.n_lat // ctx_len
    k0, v0 = n_heads, 2 * n_heads
    lam_spec = pl.BlockSpec((None,) + lam_p.shape[1:], lambda b, h, i: (layer, 0, 0))
    g_spec = pl.BlockSpec((None, 1, V_DIM), lambda b, h, i: (layer, 0, 0))
    if ctx_part:
        nq = ctx_len // tq
        q_spec = pl.BlockSpec((tq, LANES), lambda b, h, i: (ctx_blk0 * (ctx_len // tq) + b * nq + i, h))
        kv = [pl.BlockSpec((ctx_len, LANES), lambda b, h, i: (ctx_blk0 + b, k0 + h)),
              pl.BlockSpec((ctx_len, LANES), lambda b, h, i: (ctx_blk0 + b, v0 + h))]
        args = [qkv, qkv, qkv]
        n_kv = 1
    else:
        nq = seq // tq
        q_spec = pl.BlockSpec((tq, LANES), lambda b, h, i: (b * nq + i, h))
        kv = [pl.BlockSpec((seq, LANES), lambda b, h, i: (b, k0 + h)),
              pl.BlockSpec((ctx_len, LANES), lambda b, h, i: (ctx_blk0 + b, k0 + h)),
              pl.BlockSpec((seq, LANES), lambda b, h, i: (b, v0 + h)),
              pl.BlockSpec((ctx_len, LANES), lambda b, h, i: (ctx_blk0 + b, v0 + h))]
        args = [qkv] * 5
        n_kv = 2
    in_specs = [q_spec] + kv + [lam_spec, g_spec]
    args = args + [lam_p, subln_g.reshape(-1, 1, V_DIM)]
    kern = functools.partial(_diff_attn_kernel, n_kv=n_kv, lambda_init=lambda_init)
    aliases = {}
    if y_prev is not None:
        in_specs.append(pl.BlockSpec(memory_space=pl.ANY))
        args.append(y_prev)
        aliases = {len(args) - 1: 0}
        inner = kern
        kern = lambda *refs: inner(*refs[:-2], refs[-1])
    return pl.pallas_call(
        kern, name="diff_attn_ctx" if ctx_part else "diff_attn_lat",
        grid=(dims.bsz, n_heads, nq),
        in_specs=in_specs,
        out_specs=q_spec,
        out_shape=jax.ShapeDtypeStruct((dims.n_tok, n_heads * V_DIM), BF16),
        input_output_aliases=aliases,
        compiler_params=_cparams("parallel", "parallel", "parallel"),
    )(*args)


def _merge_kernel(yr_ref, ya_ref, yc_ref, g0_ref, g1_ref, g2_ref, bm_ref, wb_ref, o_ref):
    acc = None
    for nb, (y_ref, g_ref) in enumerate(((yr_ref, g0_ref), (ya_ref, g1_ref), (yc_ref, g2_ref))):
        gate = _sigmoid(g_ref[...] + bm_ref[nb:nb + 1, :])
        t = gate * _dot(y_ref[...], wb_ref[nb])
        acc = t if acc is None else acc + t
    o_ref[...] = acc.astype(BF16)


def _merge(dims, y_rnn, y_att, y_conv, u, b_merge, w_branch, layer, width, gate_col0):
    d = dims.d
    tm, tn = ROW_TILE, min(512, d)
    y_spec = pl.BlockSpec((tm, width), lambda n, m: (m, 0))

    def gate_spec(nb):
        c0 = (gate_col0 + nb * d) // tn
        return pl.BlockSpec((tm, tn), lambda n, m: (m, c0 + n))
    return pl.pallas_call(
        _merge_kernel, name="merge",
        grid=(d // tn, dims.n_tok // tm),
        in_specs=[y_spec, y_spec, y_spec, gate_spec(0), gate_spec(1), gate_spec(2),
                  pl.BlockSpec((None, 3, tn), lambda n, m: (layer, 0, n)),
                  pl.BlockSpec((None, 3, width, tn), lambda n, m: (layer, 0, 0, n))],
        out_specs=pl.BlockSpec((tm, tn), lambda n, m: (m, n)),
        out_shape=jax.ShapeDtypeStruct((dims.n_tok, d), BF16),
        compiler_params=_cparams("parallel", "parallel"),
    )(y_rnn, y_att, y_conv, u, u, u, b_merge, w_branch)


def _first_index_of_max(v, lane, big):
    m = jnp.max(v, axis=-1, keepdims=True)
    idx = jnp.min(jnp.where(v == m, lane, big), axis=-1, keepdims=True)
    return m, idx


def _route(logits, rb, n_experts):
    lane = lax.broadcasted_iota(jnp.int32, logits.shape, 1)
    valid = lane < n_experts
    per_group = n_experts // N_GROUPS
    grp = lane // per_group
    neg = -jnp.inf
    aff = _sigmoid(logits)
    sel = jnp.where(valid, aff + rb, neg)
    best_s = best_g = None
    for g in range(N_GROUPS):
        v = jnp.where(grp == g, sel, neg)
        m1, i1 = _first_index_of_max(v, lane, LANES)
        m2 = jnp.max(jnp.where(lane == i1, neg, v), axis=-1, keepdims=True)
        score = m1 + m2
        if g == 0:
            best_s, best_g = score, jnp.zeros_like(i1)
        else:
            upd = score > best_s
            best_g = jnp.where(upd, g, best_g)
            best_s = jnp.where(upd, score, best_s)
    v = jnp.where(grp == best_g, sel, neg)
    _, i1 = _first_index_of_max(v, lane, LANES)
    _, i2 = _first_index_of_max(jnp.where(lane == i1, neg, v), lane, LANES)
    a1 = jnp.sum(jnp.where(lane == i1, aff, 0.0), axis=-1, keepdims=True)
    a2 = jnp.sum(jnp.where(lane == i2, aff, 0.0), axis=-1, keepdims=True)
    tot = a1 + a2
    idx = jnp.where(lane == 0, i1, jnp.where(lane == 1, i2, 0))
    gate = jnp.where(lane == 0, a1 / tot, jnp.where(lane == 1, a2 / tot, 0.0))
    return idx, gate


def _out_proj_kernel(z_ref, w_ref, x_ref, g1_ref, ng_ref, sh_ref, sc_ref, rw_ref, rb_ref,
                     x1_ref, h2_ref, idx_ref, gate_ref, *, n_experts):
    o = _dot(z_ref[...], w_ref[...])
    x1 = x_ref[...] + g1_ref[...] * o
    x1_ref[...] = x1
    h2 = _rms_mod(x1, ng_ref[...], sh_ref[...], sc_ref[...])
    h2_ref[...] = h2.astype(BF16)
    logits = jnp.dot(h2, rw_ref[...], precision=HIGHEST, preferred_element_type=F32)
    idx, gate = _route(logits, rb_ref[...], n_experts)
    idx_ref[...] = idx
    gate_ref[...] = gate


def _out_proj(dims, z, w_out, x, norm_g, mod4, rw_pad, rb_pad, layer, n_experts):
    d = dims.d
    tm = SEQ_TILE
    row = pl.BlockSpec((tm, d), lambda m: (m, 0))
    small = pl.BlockSpec((tm, LANES), lambda m: (m, 0))
    return pl.pallas_call(
        functools.partial(_out_proj_kernel, n_experts=n_experts), name="out_proj",
        grid=(dims.n_tok // tm,),
        in_specs=[row,
                  pl.BlockSpec((None, d, d), lambda m: (layer, 0, 0), pipeline_mode=pl.Buffered(1)),
                  row,
                  _mod_spec(dims, layer, 2, tm),
                  pl.BlockSpec((None, 1, d), lambda m: (layer, 0, 0)),
                  _mod_spec(dims, layer, 3, tm),
                  _mod_spec(dims, layer, 4, tm),
                  pl.BlockSpec((d, LANES), lambda m: (0, 0)),
                  pl.BlockSpec((1, LANES), lambda m: (0, 0))],
        out_specs=[row, row, small, small],
        out_shape=[jax.ShapeDtypeStruct((dims.n_tok, d), F32),
                   jax.ShapeDtypeStruct((dims.n_tok, d), BF16),
                   jax.ShapeDtypeStruct((dims.n_tok, LANES), jnp.int32),
                   jax.ShapeDtypeStruct((dims.n_tok, LANES), F32)],
        compiler_params=_cparams("parallel"),
    )(z, w_out, x, mod4, norm_g.reshape(-1, 1, d), mod4, mod4, rw_pad, rb_pad)


def _moe_up_kernel(be_ref, nu_ref, x_ref, wg_ref, wu_ref, o_ref):
    i = pl.program_id(0)

    @pl.when(i < nu_ref[0])
    def _():
        x = x_ref[...]
        o_ref[...] = (_silu(_dot(x, wg_ref[...])) * _dot(x, wu_ref[...])).astype(BF16)

    @pl.when(i >= nu_ref[0])
    def _():
        o_ref[...] = jnp.zeros_like(o_ref)


def _moe_down_kernel(be_ref, nu_ref, h_ref, wd_ref, o_ref):
    i = pl.program_id(0)

    @pl.when(i < nu_ref[0])
    def _():
        o_ref[...] = _dot(h_ref[...], wd_ref[...])

    @pl.when(i >= nu_ref[0])
    def _():
        o_ref[...] = jnp.zeros_like(o_ref)


def _moe_experts(xs, block_expert, n_used, w_gate, w_up, w_down, layer):
    n_rows, d = xs.shape
    f = w_gate.shape[3]
    bm = MOE_ROWS
    n_blocks = n_rows // bm
    hmid = pl.pallas_call(
        _moe_up_kernel, name="moe_up",
        grid_spec=pltpu.PrefetchScalarGridSpec(
            num_scalar_prefetch=2, grid=(n_blocks,),
            in_specs=[pl.BlockSpec((bm, d), lambda i, be, nu: (i, 0)),
                      pl.BlockSpec((None, None, d, f), lambda i, be, nu: (layer, be[i], 0, 0)),
                      pl.BlockSpec((None, None, d, f), lambda i, be, nu: (layer, be[i], 0, 0))],
            out_specs=pl.BlockSpec((bm, f), lambda i, be, nu: (i, 0))),
        out_shape=jax.ShapeDtypeStruct((n_rows, f), BF16),
        compiler_params=_cparams("arbitrary"),
    )(block_expert, n_used, xs, w_gate, w_up)
    return pl.pallas_call(
        _moe_down_kernel, name="moe_down",
        grid_spec=pltpu.PrefetchScalarGridSpec(
            num_scalar_prefetch=2, grid=(n_blocks,),
            in_specs=[pl.BlockSpec((bm, f), lambda i, be, nu: (i, 0)),
                      pl.BlockSpec((None, None, f, d), lambda i, be, nu: (layer, be[i], 0, 0))],
            out_specs=pl.BlockSpec((bm, d), lambda i, be, nu: (i, 0))),
        out_shape=jax.ShapeDtypeStruct((n_rows, d), F32),
        compiler_params=_cparams("arbitrary"),
    )(block_expert, n_used, hmid, w_down)


def _dispatch(e_idx, n_experts):
    n_tok = e_idx.shape[0]
    bm = MOE_ROWS
    n_assign = n_tok * TOP_K
    n_blocks = (n_assign + n_experts * (bm - 1)) // bm
    e_flat = e_idx.reshape(n_assign)
    onehot = (e_flat[:, None] == jnp.arange(n_experts, dtype=jnp.int32)[None, :]).astype(jnp.int32)
    csum = jnp.cumsum(onehot, axis=0)
    counts = csum[-1]
    pos = jnp.sum(onehot * csum, axis=1) - 1
    padded = (counts + bm - 1) // bm * bm
    pad_ends = jnp.cumsum(padded)
    pad_starts = pad_ends - padded
    slot = (pad_starts[e_flat] + pos).astype(jnp.int32)
    row_tok = jnp.zeros(n_blocks * bm, jnp.int32).at[slot].set(jnp.arange(n_assign, dtype=jnp.int32) // TOP_K)
    n_used = (pad_ends[-1] // bm).astype(jnp.int32)
    blk = jnp.minimum(jnp.arange(n_blocks, dtype=jnp.int32), n_used - 1)
    block_expert = jnp.minimum(jnp.searchsorted(pad_ends, blk * bm, side='right'), n_experts - 1).astype(jnp.int32)
    return row_tok, slot.reshape(n_tok, TOP_K), block_expert, n_used.reshape(1)


def _combine_kernel(x_ref, y0_ref, y1_ref, gate_ref, g2_ref, ng_ref, sh_ref, sc_ref, xo_ref, ho_ref, *, final):
    gate = gate_ref[...]
    f = gate[:, 0:1] * y0_ref[...] + gate[:, 1:2] * y1_ref[...]
    x = x_ref[...] + g2_ref[...] * f
    if final:
        xo_ref[...] = x * lax.rsqrt(jnp.mean(x * x, axis=-1, keepdims=True) + EPS) * ng_ref[...]
    else:
        xo_ref[...] = x
        ho_ref[...] = _rms_mod(x, ng_ref[...], sh_ref[...], sc_ref[...]).astype(BF16)


def _combine(dims, x1, y0, y1, gate, mod4, norm_g, layer, final):
    d = dims.d
    tm = ROW_TILE
    row = pl.BlockSpec((tm, d), lambda m: (m, 0))
    nxt = layer if final else layer + 1
    ng = norm_g.reshape(-1, 1, d)
    ng_spec = pl.BlockSpec((None, 1, d), (lambda m: (0, 0, 0)) if final else (lambda m: (nxt, 0, 0)))
    in_specs = [row, row, row, pl.BlockSpec((tm, LANES), lambda m: (m, 0)),
                _mod_spec(dims, layer, 5, tm), ng_spec,
                _mod_spec(dims, nxt, 0, tm), _mod_spec(dims, nxt, 1, tm)]
    out_shape = [jax.ShapeDtypeStruct((dims.n_tok, d), F32)]
    out_specs = [row]
    kern = functools.partial(_combine_kernel, final=final)
    if final:
        inner = kern
        kern = lambda *refs: inner(*refs, None)
    else:
        out_shape.append(jax.ShapeDtypeStruct((dims.n_tok, d), BF16))
        out_specs.append(row)
    return pl.pallas_call(
        kern, name="combine_final" if final else "combine",
        grid=(dims.n_tok // tm,),
        in_specs=in_specs, out_specs=out_specs, out_shape=out_shape,
        compiler_params=_cparams("parallel"),
    )(x1, y0, y1, gate, mod4, ng, mod4, mod4)


def _pair_gate_weights(lru_w):
    depth, n_dir, n_gate, nb, bw, _ = lru_w.shape
    assert n_dir == 2 and n_gate == 2 and 2 * bw == LANES and nb % 2 == 0
    w = lru_w.reshape(depth, 4, nb // 2, 2, bw, bw)
    eye = jnp.eye(2, dtype=lru_w.dtype)
    wp = jnp.einsum('lgpjio,jk->lpjigko', w, eye)
    return wp.reshape(depth, nb // 2, 2 * bw, 4 * 2 * bw)


def _rope_tables(seq, n_identity):
    rows = seq // GRID_W
    pos_row = jnp.repeat(jnp.arange(rows), GRID_W).astype(F32)
    pos_col = jnp.tile(jnp.arange(GRID_W), rows).astype(F32)
    inv = ROPE_BASE ** (-jnp.arange(ROPE_PAIRS, dtype=F32) / ROPE_PAIRS)
    ang_r = pos_row[:, None] * inv
    ang_c = pos_col[:, None] * inv
    ang = jnp.concatenate([ang_r, ang_r, ang_c, ang_c] * 2, axis=-1)
    cos = jnp.concatenate([jnp.cos(ang), jnp.ones((n_identity, LANES), F32)], axis=0)
    sin = jnp.concatenate([jnp.sin(ang), jnp.zeros((n_identity, LANES), F32)], axis=0)
    return cos, sin


def kernel(x, c, ctx, c_ctx, ada_w, ada_b, norm1_g, norm2_g, w_in, b_merge, rnn_conv_w, rnn_conv_b, lru_w, lru_b, lru_lam, diff_lam, diff_subln_g, sconv_w, w_branch, w_out, router_w, router_b, exp_w_gate, exp_w_up, exp_w_down, final_g):
    bsz, seq, d = x.shape
    ctx_len = ctx.shape[1]
    depth = ada_w.shape[0]
    width = rnn_conv_b.shape[1]
    n_experts = router_w.shape[1]
    d_in = w_in.shape[2]
    n_heads = (d_in - 5 * width - 3 * d) // (3 * V_DIM)
    assert n_heads * V_DIM == width and d % width == 0 and d_in == 8 * width + 3 * d
    dims = _Dims(bsz, seq, ctx_len, d)

    xt = jnp.concatenate([x.reshape(bsz * seq, d), ctx.reshape(bsz * ctx_len, d)], axis=0)
    cvec = jnp.zeros((SUBLANES, d), F32).at[:bsz].set(c).at[bsz].set(c_ctx)
    w_pair = _pair_gate_weights(lru_w)
    gate_b = lru_b.reshape(depth, 4, width)
    rw_pad = jnp.zeros((d, LANES), F32).at[:, :n_experts].set(router_w)
    rb_pad = jnp.zeros((1, LANES), F32).at[0, :n_experts].set(router_b)
    cos_t, sin_t = _rope_tables(seq, ROW_TILE)

    mod = _ada_modulation(cvec, ada_w, ada_b)
    mod4 = mod.reshape(depth, SUBLANES, 1, 6 * d)
    zeros_h0 = jnp.zeros((bsz, SUBLANES, width), F32)

    h = _norm_mod(dims, xt, norm1_g, mod4, 0)
    for layer in range(depth):
        last = layer == depth - 1
        lambda_init = 0.8 - 0.6 * math.exp(-0.3 * layer)
        u = _in_proj_f32(dims, h, w_in, layer, width, 2, 3)
        qkv = _in_proj_qkv(dims, h, w_in, cos_t, sin_t, layer, width, 2)
        coef = _lru_coefs(dims, u, rnn_conv_w, rnn_conv_b, w_pair, gate_b, lru_lam, layer, width)
        y_rnn, h_ctx = _lru_scan(dims, coef, u, zeros_h0, None, width, 1, True)
        y_rnn, _ = _lru_scan(dims, coef, u, h_ctx, y_rnn, width, 1, False)
        y_att = _diff_attn(dims, qkv, diff_lam, diff_subln_g, None, layer, n_heads, lambda_init, False)
        y_att = _diff_attn(dims, qkv, diff_lam, diff_subln_g, y_att, layer, n_heads, lambda_init, True)
        y_conv = _short_conv(dims, u, sconv_w, layer, width, 2)
        z = _merge(dims, y_rnn, y_att, y_conv, u, b_merge, w_branch, layer, width, 5 * width)
        x1, h2, e_idx, gate = _out_proj(dims, z, w_out, xt, norm2_g, mod4, rw_pad, rb_pad, layer, n_experts)
        row_tok, slot, block_expert, n_used = _dispatch(e_idx[:, :TOP_K], n_experts)
        xs = jnp.take(h2, row_tok, axis=0)
        y_rows = _moe_experts(xs, block_expert, n_used, exp_w_gate, exp_w_up, exp_w_down, layer)
        y0 = jnp.take(y_rows, slot[:, 0], axis=0)
        y1 = jnp.take(y_rows, slot[:, 1], axis=0)
        if last:
            (xt,) = _combine(dims, x1, y0, y1, gate, mod4, final_g, layer, True)
        else:
            xt, h = _combine(dims, x1, y0, y1, gate, mod4, norm1_g, layer, False)
    return xt[:bsz * seq].reshape(bsz, seq, d)
```

```python
import functools
import math

import jax
import jax.numpy as jnp
from jax import lax
from jax.experimental import pallas as pl
from jax.experimental.pallas import tpu as pltpu

F32 = jnp.float32
BF16 = jnp.bfloat16

GRID_W = 64
HEAD_DIM = 64
V_DIM = 2 * HEAD_DIM
ROPE_PAIRS = HEAD_DIM // 4
ROPE_BASE = 10000.0
LRU_C = 8.0
RNN_PAD_LEFT = 2
SCONV_PAD_LEFT = 1
N_GROUPS = 4
TOP_K = 2
EPS = 1e-6

LANES = 128
SUBLANES = 8
V7X_VMEM_LIMIT_BYTES = 56 * 1024 * 1024

ROW_TILE = 512
SEQ_TILE = 256
Q_TILE = 256
Q_SUBTILES = 1
MOE_ROWS = 256
HIGHEST = lax.Precision.HIGHEST


def _cparams(*sem):
    return pltpu.CompilerParams(dimension_semantics=sem, vmem_limit_bytes=V7X_VMEM_LIMIT_BYTES)


def _sigmoid(x):
    return 1.0 / (1.0 + jnp.exp(-x))


def _silu(x):
    return x * _sigmoid(x)


def _gelu_tanh(x):
    return 0.5 * x * (1.0 + jnp.tanh(math.sqrt(2.0 / math.pi) * (x + 0.044715 * (x * x * x))))


def _dot(a, b):
    return lax.dot_general(a, b, (((1,), (0,)), ((), ())), preferred_element_type=F32)


def _dot_nt(a, b):
    return lax.dot_general(a, b, (((1,), (1,)), ((), ())), preferred_element_type=F32)


def _ada_kernel(c_ref, w_ref, b_ref, o_ref):
    cv = c_ref[...]
    o_ref[0] = jnp.dot(_silu(cv), w_ref[0], precision=HIGHEST, preferred_element_type=F32) + b_ref[0]


def _ada_modulation(cvec, ada_w, ada_b):
    depth, d, six_d = ada_w.shape
    tn = 1536 if six_d % 1536 == 0 else LANES
    return pl.pallas_call(
        _ada_kernel, name="ada_mod",
        grid=(depth, six_d // tn),
        in_specs=[pl.BlockSpec((SUBLANES, d), lambda l, n: (0, 0)),
                  pl.BlockSpec((1, d, tn), lambda l, n: (l, 0, n)),
                  pl.BlockSpec((1, 1, tn), lambda l, n: (l, 0, n))],
        out_specs=pl.BlockSpec((1, SUBLANES, tn), lambda l, n: (l, 0, n)),
        out_shape=jax.ShapeDtypeStruct((depth, SUBLANES, six_d), F32),
        compiler_params=_cparams("parallel", "parallel"),
    )(cvec, ada_w, ada_b.reshape(depth, 1, six_d))


class _Dims:
    def __init__(self, bsz, seq, ctx_len, d):
        self.bsz, self.seq, self.ctx_len, self.d = bsz, seq, ctx_len, d
        self.n_lat = bsz * seq
        self.n_ctx = bsz * ctx_len
        self.n_tok = self.n_lat + self.n_ctx
        assert seq % ROW_TILE == 0 and self.n_ctx % ROW_TILE == 0
        assert seq % SEQ_TILE == 0 and ctx_len % SEQ_TILE == 0
        assert seq % Q_TILE == 0 and ctx_len % Q_TILE == 0 and seq % ctx_len == 0

    def seg(self, m, tile):
        return jnp.where(m < self.n_lat // tile, m // (self.seq // tile), self.bsz)


def _mod_spec(dims, layer, j, tile, grid_rank=1, m_axis=0):
    d = dims.d

    def imap(*ids):
        return (layer, dims.seg(ids[m_axis], tile), 0, j)
    return pl.BlockSpec((None, None, 1, d), imap)


def _rms_mod(x, g, shift, scale):
    y = x * lax.rsqrt(jnp.mean(x * x, axis=-1, keepdims=True) + EPS) * g
    return y * (1.0 + scale) + shift


def _norm_mod_kernel(x_ref, g_ref, sh_ref, sc_ref, h_ref):
    h_ref[...] = _rms_mod(x_ref[...], g_ref[...], sh_ref[...], sc_ref[...]).astype(BF16)


def _norm_mod(dims, x, norm_g, mod4, layer):
    d = dims.d
    tm = ROW_TILE
    return pl.pallas_call(
        _norm_mod_kernel, name="norm_mod",
        grid=(dims.n_tok // tm,),
        in_specs=[pl.BlockSpec((tm, d), lambda m: (m, 0)),
                  pl.BlockSpec((None, 1, d), lambda m: (layer, 0, 0)),
                  _mod_spec(dims, layer, 0, tm),
                  _mod_spec(dims, layer, 1, tm)],
        out_specs=pl.BlockSpec((tm, d), lambda m: (m, 0)),
        out_shape=jax.ShapeDtypeStruct((dims.n_tok, d), BF16),
        compiler_params=_cparams("parallel"),
    )(x, norm_g.reshape(-1, 1, d), mod4, mod4)


def _in_proj_f32_kernel(a_ref, w_ref, o_ref):
    o_ref[...] = _dot(a_ref[...], w_ref[...])


def _in_proj_f32(dims, h, w_in, layer, width, n_lead, n_skip):
    d = dims.d
    tm, tn = ROW_TILE, width
    n_tiles = w_in.shape[2] // tn - n_skip
    return pl.pallas_call(
        _in_proj_f32_kernel, name="in_proj_f32",
        grid=(n_tiles, dims.n_tok // tm),
        in_specs=[pl.BlockSpec((tm, d), lambda n, m: (m, 0)),
                  pl.BlockSpec((None, d, tn), lambda n, m: (layer, 0, jnp.where(n < n_lead, n, n + n_skip)))],
        out_specs=pl.BlockSpec((tm, tn), lambda n, m: (m, n)),
        out_shape=jax.ShapeDtypeStruct((dims.n_tok, n_tiles * tn), F32),
        compiler_params=_cparams("parallel", "parallel"),
    )(h, w_in)


def _rotate_half_pairs(x):
    n = x.shape[-1]
    lane = lax.broadcasted_iota(jnp.int32, x.shape, x.ndim - 1)
    first = (lane % (2 * ROPE_PAIRS)) < ROPE_PAIRS
    return jnp.where(first, -pltpu.roll(x, n - ROPE_PAIRS, x.ndim - 1), pltpu.roll(x, ROPE_PAIRS, x.ndim - 1))


def _in_proj_qkv_kernel(a_ref, w_ref, cos_ref, sin_ref, o_ref, *, reps):
    n = pl.program_id(0)
    acc = _dot(a_ref[...], w_ref[...])

    @pl.when(n < 2)
    def _():
        cos = jnp.concatenate([cos_ref[...]] * reps, axis=1)
        sin = jnp.concatenate([sin_ref[...]] * reps, axis=1)
        scale = jnp.where(n == 0, HEAD_DIM ** -0.5, 1.0).astype(F32)
        o_ref[...] = ((acc * cos + _rotate_half_pairs(acc) * sin) * scale).astype(BF16)

    @pl.when(n == 2)
    def _():
        o_ref[...] = acc.astype(BF16)


def _in_proj_qkv(dims, h, w_in, cos_t, sin_t, layer, width, col_tile0):
    d = dims.d
    tm, tn = ROW_TILE, width
    lat_tiles, per_seq = dims.n_lat // tm, dims.seq // tm

    def tab(n, m):
        return (jnp.where(m < lat_tiles, m % per_seq, per_seq), 0)
    return pl.pallas_call(
        functools.partial(_in_proj_qkv_kernel, reps=tn // LANES), name="in_proj_qkv",
        grid=(3, dims.n_tok // tm),
        in_specs=[pl.BlockSpec((tm, d), lambda n, m: (m, 0)),
                  pl.BlockSpec((None, d, tn), lambda n, m: (layer, 0, col_tile0 + n)),
                  pl.BlockSpec((tm, LANES), tab),
                  pl.BlockSpec((tm, LANES), tab)],
        out_specs=pl.BlockSpec((tm, tn), lambda n, m: (m, n)),
        out_shape=jax.ShapeDtypeStruct((dims.n_tok, 3 * tn), BF16),
        compiler_params=_cparams("parallel", "parallel"),
    )(h, w_in, cos_t, sin_t)


def _seq_flags(dims, m):
    lat_tiles, per_seq = dims.n_lat // SEQ_TILE, dims.seq // SEQ_TILE
    per_ctx = dims.ctx_len // SEQ_TILE
    in_lat = m < lat_tiles
    pos = jnp.where(in_lat, m % per_seq, (m - lat_tiles) % per_ctx)
    n = jnp.where(in_lat, per_seq, per_ctx)
    return pos == 0, pos == n - 1


def _halo_specs(dims, col_tile, width):
    r = SEQ_TILE // SUBLANES
    last = dims.n_tok // SUBLANES - 1
    return [pl.BlockSpec((SEQ_TILE, width), lambda m: (m, col_tile)),
            pl.BlockSpec((SUBLANES, width), lambda m: (jnp.maximum(m * r - 1, 0), col_tile)),
            pl.BlockSpec((SUBLANES, width), lambda m: (jnp.minimum((m + 1) * r, last), col_tile))]


def _dwconv_tile(cur, prev, nxt, first, last, w_ref, pad_left):
    rows = cur.shape[0]
    prev = jnp.where(first, 0.0, prev)
    nxt = jnp.where(last, 0.0, nxt)
    xp = jnp.concatenate([prev, cur, nxt], axis=0)
    out = None
    for k in range(w_ref.shape[0]):
        off = SUBLANES + k - pad_left
        term = w_ref[k:k + 1, :] * xp[off:off + rows, :]
        out = term if out is None else out + term
    return out


def _lru_coef_kernel(cur_ref, prev_ref, next_ref, cw_ref, cb_ref, wp_ref, gb_ref, lam_ref, o_ref, *, dims):
    first, last = _seq_flags(dims, pl.program_id(0))
    xc = _dwconv_tile(cur_ref[...], prev_ref[...], next_ref[...], first, last, cw_ref, RNN_PAD_LEFT) + cb_ref[...]
    lam = lam_ref[...]
    neg_softplus = -(jnp.maximum(-lam, 0.0) + jnp.log1p(jnp.exp(-jnp.abs(lam))))
    for p in range(wp_ref.shape[0]):
        sl = slice(p * LANES, (p + 1) * LANES)
        xg = xc[:, sl]
        g = _dot(xg, wp_ref[p])
        for dr in range(2):
            c0 = 2 * dr * LANES
            r = _sigmoid(g[:, c0:c0 + LANES] + gb_ref[2 * dr:2 * dr + 1, sl])
            i = _sigmoid(g[:, c0 + LANES:c0 + 2 * LANES] + gb_ref[2 * dr + 1:2 * dr + 2, sl])
            log_a = LRU_C * r * neg_softplus[dr:dr + 1, sl]
            o_ref[2 * dr, :, sl] = jnp.exp(log_a)
            th = jnp.tanh(log_a)
            o_ref[2 * dr + 1, :, sl] = jnp.sqrt(-2.0 * th / (1.0 - th)) * (i * xg)


def _lru_coefs(dims, u, conv_w, conv_b, w_pair, gate_b, lam, layer, width):
    n_pair = width // LANES
    return pl.pallas_call(
        functools.partial(_lru_coef_kernel, dims=dims), name="lru_coefs",
        grid=(dims.n_tok // SEQ_TILE,),
        in_specs=_halo_specs(dims, 0, width) + [
            pl.BlockSpec((None,) + conv_w.shape[1:], lambda m: (layer, 0, 0)),
            pl.BlockSpec((None, 1, width), lambda m: (layer, 0, 0)),
            pl.BlockSpec((None, n_pair, LANES, 4 * LANES), lambda m: (layer, 0, 0, 0)),
            pl.BlockSpec((None, 4, width), lambda m: (layer, 0, 0)),
            pl.BlockSpec((None, 2, width), lambda m: (layer, 0, 0))],
        out_specs=pl.BlockSpec((4, SEQ_TILE, width), lambda m: (0, m, 0)),
        out_shape=jax.ShapeDtypeStruct((4, dims.n_tok, width), F32),
        compiler_params=_cparams("parallel"),
    )(u, u, u, conv_w, conv_b.reshape(-1, 1, width), w_pair, gate_b, lam)


def _short_conv_kernel(b_ref, c_ref, cp_ref, cn_ref, x_ref, xp_ref, xn_ref, w_ref, o_ref, *, dims):
    first, last = _seq_flags(dims, pl.program_id(0))
    conv = _dwconv_tile(c_ref[...] * x_ref[...], cp_ref[...] * xp_ref[...], cn_ref[...] * xn_ref[...],
                        first, last, w_ref, SCONV_PAD_LEFT)
    o_ref[...] = (b_ref[...] * conv).astype(BF16)


def _short_conv(dims, u, sconv_w, layer, width, col_tile0):
    cur_b = pl.BlockSpec((SEQ_TILE, width), lambda m: (m, col_tile0))
    return pl.pallas_call(
        functools.partial(_short_conv_kernel, dims=dims), name="short_conv",
        grid=(dims.n_tok // SEQ_TILE,),
        in_specs=[cur_b] + _halo_specs(dims, col_tile0 + 1, width) + _halo_specs(dims, col_tile0 + 2, width) + [
            pl.BlockSpec((None,) + sconv_w.shape[1:], lambda m: (layer, 0, 0))],
        out_specs=pl.BlockSpec((SEQ_TILE, width), lambda m: (m, 0)),
        out_shape=jax.ShapeDtypeStruct((dims.n_tok, width), BF16),
        compiler_params=_cparams("parallel"),
    )(u, u, u, u, u, u, u, sconv_w)


def _scan_chunk(a, b, reverse):
    row = lax.broadcasted_iota(jnp.int32, a.shape, 0)
    for s in (1, 2, 4):
        shift = SUBLANES - s if reverse else s
        keep = (row < SUBLANES - s) if reverse else (row >= s)
        a_sh = pltpu.roll(a, shift, 0)
        b_sh = pltpu.roll(b, shift, 0)
        b = jnp.where(keep, a * b_sh + b, b)
        a = jnp.where(keep, a * a_sh, a)
    return a, b


def _lru_scan_kernel(coef_ref, rg_ref, h0_ref, yin_ref, y_ref, hfin_ref, hf_scr, hr_scr, *, n_chunks):
    del yin_ref

    def body(i, carry):
        cf, cr = carry
        rf = pl.multiple_of(i * SUBLANES, SUBLANES)
        rr = pl.multiple_of((n_chunks - 1 - i) * SUBLANES, SUBLANES)
        af, bf = _scan_chunk(coef_ref[0, pl.ds(rf, SUBLANES), :], coef_ref[1, pl.ds(rf, SUBLANES), :], False)
        ar, br = _scan_chunk(coef_ref[2, pl.ds(rr, SUBLANES), :], coef_ref[3, pl.ds(rr, SUBLANES), :], True)
        hf = af * cf + bf
        hr = ar * cr + br
        hf_scr[pl.ds(rf, SUBLANES), :] = hf
        hr_scr[pl.ds(rr, SUBLANES), :] = hr
        return hf[SUBLANES - 1:SUBLANES, :], hr[0:1, :]

    cf, cr = lax.fori_loop(0, n_chunks, body, (h0_ref[0:1, :], h0_ref[1:2, :]), unroll=4)
    row = lax.broadcasted_iota(jnp.int32, hfin_ref.shape, 0)
    hfin_ref[...] = jnp.where(row == 0, cf, jnp.where(row == 1, cr, 0.0))
    y_ref[...] = ((hf_scr[...] + hr_scr[...]) * _gelu_tanh(rg_ref[...])).astype(BF16)


def _lru_scan(dims, coef, u, h0, y_prev, width, rg_col_tile, ctx_part):
    rows = dims.ctx_len if ctx_part else dims.seq
    row0 = dims.n_lat // rows if ctx_part else 0
    tc = LANES
    n_ct = width // tc
    rg0 = rg_col_tile * n_ct
    y_shape = jax.ShapeDtypeStruct((dims.n_tok, width), BF16)
    fin_shape = jax.ShapeDtypeStruct((dims.bsz, SUBLANES, width), F32)
    in_specs = [pl.BlockSpec((4, rows, tc), lambda b, c: (0, row0 + b, c)),
                pl.BlockSpec((rows, tc), lambda b, c: (row0 + b, rg0 + c)),
                pl.BlockSpec((None, SUBLANES, tc), lambda b, c: (b, 0, c))]
    args = [coef, u, h0]
    aliases = {}
    kern = functools.partial(_lru_scan_kernel, n_chunks=rows // SUBLANES)
    if y_prev is not None:
        in_specs.append(pl.BlockSpec(memory_space=pl.ANY))
        args.append(y_prev)
        aliases = {3: 0}
    else:
        kern = functools.partial(lambda *refs, **kw: _lru_scan_kernel(*refs[:3], None, *refs[3:], **kw),
                                 n_chunks=rows // SUBLANES)
    return pl.pallas_call(
        kern, name="lru_scan_ctx" if ctx_part else "lru_scan_lat",
        grid=(dims.bsz, n_ct),
        in_specs=in_specs,
        out_specs=[pl.BlockSpec((rows, tc), lambda b, c: (row0 + b, c)),
                   pl.BlockSpec((None, SUBLANES, tc), lambda b, c: (b, 0, c))],
        out_shape=[y_shape, fin_shape],
        scratch_shapes=[pltpu.VMEM((rows, tc), F32), pltpu.VMEM((rows, tc), F32)],
        input_output_aliases=aliases,
        compiler_params=_cparams("parallel", "parallel"),
    )(*args)


def _diff_lambda(lam_ref, lambda_init):
    lp = lam_ref[...]
    l1 = jnp.exp(jnp.sum(lp[0:1, :] * lp[1:2, :], axis=-1, keepdims=True))
    l2 = jnp.exp(jnp.sum(lp[2:3, :] * lp[3:4, :], axis=-1, keepdims=True))
    return l1 - l2 + lambda_init


def _attend(q, k_refs, v_refs, lam, gain):
    tq = q.shape[0]
    lane = lax.broadcasted_iota(jnp.int32, q.shape, 1)
    zero = jnp.zeros_like(q)
    q2 = jnp.concatenate([jnp.where(lane < HEAD_DIM, q, zero), jnp.where(lane >= HEAD_DIM, q, zero)], axis=0)
    s = [_dot_nt(q2, k[...]) for k in k_refs]
    m = functools.reduce(jnp.maximum, [jnp.max(x, axis=-1, keepdims=True) for x in s])
    p = [jnp.exp(x - m) for x in s]
    denom = functools.reduce(jnp.add, [jnp.sum(x, axis=-1, keepdims=True) for x in p])
    inv = 1.0 / denom
    c1 = inv[:tq]
    c2 = inv[tq:] * lam
    o = None
    for x, v in zip(p, v_refs):
        w = (x[:tq] * c1 - x[tq:] * c2).astype(BF16)
        t = _dot(w, v[...])
        o = t if o is None else o + t
    return o * lax.rsqrt(jnp.mean(o * o, axis=-1, keepdims=True) + EPS) * gain


def _diff_attn_kernel(*refs, n_kv, lambda_init, n_sub):
    q_ref = refs[0]
    k_refs = refs[1:1 + n_kv]
    v_refs = refs[1 + n_kv:1 + 2 * n_kv]
    lam_ref, g_ref = refs[1 + 2 * n_kv:3 + 2 * n_kv]
    o_ref = refs[-1]
    lam = _diff_lambda(lam_ref, lambda_init)
    gain = g_ref[...] * (1.0 - lambda_init)
    rows = q_ref.shape[0] // n_sub
    for j in range(n_sub):
        sl = slice(j * rows, (j + 1) * rows)
        o_ref[sl, :] = _attend(q_ref[sl, :], k_refs, v_refs, lam, gain).astype(BF16)


def _diff_attn(dims, qkv, lam_p, subln_g, y_prev, layer, n_heads, lambda_init, ctx_part):
    tq = Q_TILE
    seq, ctx_len = dims.seq, dims.ctx_len
    ctx_blk0 = dims.n_lat // ctx_len
    k0, v0 = n_heads, 2 * n_heads
    lam_spec = pl.BlockSpec((None,) + lam_p.shape[1:], lambda b, h, i: (layer, 0, 0))
    g_spec = pl.BlockSpec((None, 1, V_DIM), lambda b, h, i: (layer, 0, 0))
    if ctx_part:
        nq = ctx_len // tq
        q_spec = pl.BlockSpec((tq, LANES), lambda b, h, i: (ctx_blk0 * (ctx_len // tq) + b * nq + i, h))
        kv = [pl.BlockSpec((ctx_len, LANES), lambda b, h, i: (ctx_blk0 + b, k0 + h)),
              pl.BlockSpec((ctx_len, LANES), lambda b, h, i: (ctx_blk0 + b, v0 + h))]
        args = [qkv, qkv, qkv]
        n_kv = 1
    else:
        nq = seq // tq
        q_spec = pl.BlockSpec((tq, LANES), lambda b, h, i: (b * nq + i, h))
        kv = [pl.BlockSpec((seq, LANES), lambda b, h, i: (b, k0 + h)),
              pl.BlockSpec((ctx_len, LANES), lambda b, h, i: (ctx_blk0 + b, k0 + h)),
              pl.BlockSpec((seq, LANES), lambda b, h, i: (b, v0 + h)),
              pl.BlockSpec((ctx_len, LANES), lambda b, h, i: (ctx_blk0 + b, v0 + h))]
        args = [qkv] * 5
        n_kv = 2
    in_specs = [q_spec] + kv + [lam_spec, g_spec]
    args = args + [lam_p, subln_g.reshape(-1, 1, V_DIM)]
    kern = functools.partial(_diff_attn_kernel, n_kv=n_kv, lambda_init=lambda_init, n_sub=Q_SUBTILES)
    aliases = {}
    if y_prev is not None:
        in_specs.append(pl.BlockSpec(memory_space=pl.ANY))
        args.append(y_prev)
        aliases = {len(args) - 1: 0}
        inner = kern
        kern = lambda *refs: inner(*refs[:-2], refs[-1])
    return pl.pallas_call(
        kern, name="diff_attn_ctx" if ctx_part else "diff_attn_lat",
        grid=(dims.bsz, n_heads, nq),
        in_specs=in_specs,
        out_specs=q_spec,
        out_shape=jax.ShapeDtypeStruct((dims.n_tok, n_heads * V_DIM), BF16),
        input_output_aliases=aliases,
        compiler_params=_cparams("parallel", "parallel", "parallel"),
    )(*args)


def _merge_kernel(yr_ref, ya_ref, yc_ref, g0_ref, g1_ref, g2_ref, bm_ref, wb_ref, o_ref):
    acc = None
    for nb, (y_ref, g_ref) in enumerate(((yr_ref, g0_ref), (ya_ref, g1_ref), (yc_ref, g2_ref))):
        gate = _sigmoid(g_ref[...] + bm_ref[nb:nb + 1, :])
        t = gate * _dot(y_ref[...], wb_ref[nb])
        acc = t if acc is None else acc + t
    o_ref[...] = acc.astype(BF16)


def _merge(dims, y_rnn, y_att, y_conv, u, b_merge, w_branch, layer, width, gate_col0):
    d = dims.d
    tm, tn = ROW_TILE, min(512, d)
    y_spec = pl.BlockSpec((tm, width), lambda n, m: (m, 0))

    def gate_spec(nb):
        c0 = (gate_col0 + nb * d) // tn
        return pl.BlockSpec((tm, tn), lambda n, m: (m, c0 + n))
    return pl.pallas_call(
        _merge_kernel, name="merge",
        grid=(d // tn, dims.n_tok // tm),
        in_specs=[y_spec, y_spec, y_spec, gate_spec(0), gate_spec(1), gate_spec(2),
                  pl.BlockSpec((None, 3, tn), lambda n, m: (layer, 0, n)),
                  pl.BlockSpec((None, 3, width, tn), lambda n, m: (layer, 0, 0, n))],
        out_specs=pl.BlockSpec((tm, tn), lambda n, m: (m, n)),
        out_shape=jax.ShapeDtypeStruct((dims.n_tok, d), BF16),
        compiler_params=_cparams("parallel", "parallel"),
    )(y_rnn, y_att, y_conv, u, u, u, b_merge, w_branch)


def _first_index_of_max(v, lane, big):
    m = jnp.max(v, axis=-1, keepdims=True)
    idx = jnp.min(jnp.where(v == m, lane, big), axis=-1, keepdims=True)
    return m, idx


def _route(logits, rb, n_experts):
    lane = lax.broadcasted_iota(jnp.int32, logits.shape, 1)
    valid = lane < n_experts
    per_group = n_experts // N_GROUPS
    grp = lane // per_group
    neg = -jnp.inf
    aff = _sigmoid(logits)
    sel = jnp.where(valid, aff + rb, neg)
    best_s = best_g = None
    for g in range(N_GROUPS):
        v = jnp.where(grp == g, sel, neg)
        m1, i1 = _first_index_of_max(v, lane, LANES)
        m2 = jnp.max(jnp.where(lane == i1, neg, v), axis=-1, keepdims=True)
        score = m1 + m2
        if g == 0:
            best_s, best_g = score, jnp.zeros_like(i1)
        else:
            upd = score > best_s
            best_g = jnp.where(upd, g, best_g)
            best_s = jnp.where(upd, score, best_s)
    v = jnp.where(grp == best_g, sel, neg)
    _, i1 = _first_index_of_max(v, lane, LANES)
    _, i2 = _first_index_of_max(jnp.where(lane == i1, neg, v), lane, LANES)
    a1 = jnp.sum(jnp.where(lane == i1, aff, 0.0), axis=-1, keepdims=True)
    a2 = jnp.sum(jnp.where(lane == i2, aff, 0.0), axis=-1, keepdims=True)
    tot = a1 + a2
    return i1, i2, a1 / tot, a2 / tot


def _dot_bf16x3(a, w):
    a_hi = a.astype(BF16)
    a_lo = (a - a_hi.astype(F32)).astype(BF16)
    w_hi = w.astype(BF16)
    w_lo = (w - w_hi.astype(F32)).astype(BF16)
    n = w.shape[1]
    hi = _dot(a_hi, jnp.concatenate([w_hi, w_lo], axis=1))
    return hi[:, :n] + hi[:, n:] + _dot(a_lo, w_hi)


def _out_proj_kernel(z_ref, w_ref, x_ref, g1_ref, ng_ref, sh_ref, sc_ref, rw_ref, rb_ref,
                     x1_ref, h2_ref, route_ref, gate_ref, cnt_ref, *, n_experts):
    @pl.when(pl.program_id(0) == 0)
    def _():
        cnt_ref[...] = jnp.zeros_like(cnt_ref)

    o = _dot(z_ref[...], w_ref[...])
    x1 = x_ref[...] + g1_ref[...] * o
    x1_ref[...] = x1
    h2 = _rms_mod(x1, ng_ref[...], sh_ref[...], sc_ref[...])
    h2_ref[...] = h2
    logits = _dot_bf16x3(h2, rw_ref[...])
    i1, i2, g1, g2 = _route(logits, rb_ref[...], n_experts)
    rows = logits.shape[0]
    lane = lax.broadcasted_iota(jnp.int32, logits.shape, 1)
    hit1, hit2 = lane == i1, lane == i2
    member = jnp.where(hit1 | hit2, 1.0, 0.0)
    r_i = lax.broadcasted_iota(jnp.int32, (rows, rows), 0)
    c_i = lax.broadcasted_iota(jnp.int32, (rows, rows), 1)
    before = jnp.where(c_i < r_i, 1.0, 0.0).astype(BF16)
    rank = _dot(before, member.astype(BF16)) + cnt_ref[0:1, :]
    p1 = jnp.sum(jnp.where(hit1, rank, 0.0), axis=-1, keepdims=True).astype(jnp.int32)
    p2 = jnp.sum(jnp.where(hit2, rank, 0.0), axis=-1, keepdims=True).astype(jnp.int32)
    route_ref[...] = jnp.where(lane == 0, i1, jnp.where(lane == 1, i2, jnp.where(lane == 2, p1,
                               jnp.where(lane == 3, p2, 0))))
    gate_ref[...] = jnp.where(lane == 0, g1, jnp.where(lane == 1, g2, 0.0))
    cnt_ref[...] = cnt_ref[...] + jnp.sum(member, axis=0, keepdims=True)


def _out_proj(dims, z, w_out, x, norm_g, mod4, rw_pad, rb_pad, layer, n_experts):
    d = dims.d
    tm = SEQ_TILE
    row = pl.BlockSpec((tm, d), lambda m: (m, 0))
    small = pl.BlockSpec((tm, LANES), lambda m: (m, 0))
    return pl.pallas_call(
        functools.partial(_out_proj_kernel, n_experts=n_experts), name="out_proj",
        grid=(dims.n_tok // tm,),
        in_specs=[row,
                  pl.BlockSpec((None, d, d), lambda m: (layer, 0, 0), pipeline_mode=pl.Buffered(1)),
                  row,
                  _mod_spec(dims, layer, 2, tm),
                  pl.BlockSpec((None, 1, d), lambda m: (layer, 0, 0)),
                  _mod_spec(dims, layer, 3, tm),
                  _mod_spec(dims, layer, 4, tm),
                  pl.BlockSpec((d, LANES), lambda m: (0, 0)),
                  pl.BlockSpec((1, LANES), lambda m: (0, 0))],
        out_specs=[row, row, small, small, pl.BlockSpec((SUBLANES, LANES), lambda m: (0, 0))],
        out_shape=[jax.ShapeDtypeStruct((dims.n_tok, d), F32),
                   jax.ShapeDtypeStruct((dims.n_tok, d), F32),
                   jax.ShapeDtypeStruct((dims.n_tok, LANES), jnp.int32),
                   jax.ShapeDtypeStruct((dims.n_tok, LANES), F32),
                   jax.ShapeDtypeStruct((SUBLANES, LANES), F32)],
        compiler_params=_cparams("arbitrary"),
    )(z, w_out, x, mod4, norm_g.reshape(-1, 1, d), mod4, mod4, rw_pad, rb_pad)


def _row_copy(src, src_row, dst, dst_row, sem):
    return pltpu.make_async_copy(src.at[pl.ds(src_row, 1), :], dst.at[pl.ds(dst_row, 1), :], sem)


def _moe_up_kernel(be_ref, nu_ref, src_ref, src_next_ref, h_hbm, wg_ref, wu_ref, o_ref, xbuf, sem):
    i = pl.program_id(0)
    n_used = nu_ref[0]
    bm = xbuf.shape[1]

    def gather(idx_ref, slot):
        def body(r, carry):
            _row_copy(h_hbm, idx_ref[0, r], xbuf.at[slot], r, sem.at[slot]).start()
            return carry
        lax.fori_loop(0, bm, body, 0, unroll=8)

    @pl.when(i == 0)
    def _():
        gather(src_ref, 0)

    @pl.when(i + 1 < n_used)
    def _():
        gather(src_next_ref, (i + 1) % 2)

    @pl.when(i < n_used)
    def _():
        slot = i % 2
        pltpu.make_async_copy(h_hbm.at[pl.ds(0, bm), :], xbuf.at[slot], sem.at[slot]).wait()
        x = xbuf[slot]
        o_ref[...] = (_silu(_dot(x, wg_ref[...])) * _dot(x, wu_ref[...])).astype(BF16)

    @pl.when(i >= n_used)
    def _():
        o_ref[...] = jnp.zeros_like(o_ref)


def _moe_down_kernel(be_ref, nu_ref, dst_ref, h_ref, wd_ref, y_hbm, ybuf, sem, *, n_blocks):
    i = pl.program_id(0)
    n_used = nu_ref[0]
    bm = ybuf.shape[1]
    slot = i % 2

    def wait_block(s):
        pltpu.make_async_copy(ybuf.at[s], y_hbm.at[pl.ds(0, bm), :], sem.at[s]).wait()

    @pl.when(i < n_used)
    def _():
        ybuf[slot] = _dot(h_ref[...], wd_ref[...])

    @pl.when((i >= 1) & (i <= n_used))
    def _():
        wait_block(1 - slot)

    @pl.when(i < n_used)
    def _():
        def body(r, carry):
            _row_copy(ybuf.at[slot], r, y_hbm, dst_ref[0, r], sem.at[slot]).start()
            return carry
        lax.fori_loop(0, bm, body, 0, unroll=8)

    @pl.when((i == n_blocks - 1) & (i < n_used))
    def _():
        wait_block(slot)


def _moe_experts(h2, src_tok, dst_row, block_expert, n_used, w_gate, w_up, w_down, layer, n_out_rows):
    n_tok, d = h2.shape
    f = w_gate.shape[3]
    bm = MOE_ROWS
    n_blocks = src_tok.shape[0]
    idx_spec = pl.BlockSpec((None, 1, bm), lambda i, be, nu: (i, 0, 0), memory_space=pltpu.SMEM)
    idx_next_spec = pl.BlockSpec((None, 1, bm), lambda i, be, nu: (jnp.minimum(i + 1, n_blocks - 1), 0, 0),
                                 memory_space=pltpu.SMEM)
    hmid = pl.pallas_call(
        _moe_up_kernel, name="moe_up",
        grid_spec=pltpu.PrefetchScalarGridSpec(
            num_scalar_prefetch=2, grid=(n_blocks,),
            in_specs=[idx_spec, idx_next_spec,
                      pl.BlockSpec(memory_space=pl.ANY),
                      pl.BlockSpec((None, None, d, f), lambda i, be, nu: (layer, be[i], 0, 0)),
                      pl.BlockSpec((None, None, d, f), lambda i, be, nu: (layer, be[i], 0, 0))],
            out_specs=pl.BlockSpec((bm, f), lambda i, be, nu: (i, 0)),
            scratch_shapes=[pltpu.VMEM((2, bm, d), F32), pltpu.SemaphoreType.DMA((2,))]),
        out_shape=jax.ShapeDtypeStruct((n_blocks * bm, f), BF16),
        compiler_params=_cparams("arbitrary"),
    )(block_expert, n_used, src_tok, src_tok, h2, w_gate, w_up)
    return pl.pallas_call(
        functools.partial(_moe_down_kernel, n_blocks=n_blocks), name="moe_down",
        grid_spec=pltpu.PrefetchScalarGridSpec(
            num_scalar_prefetch=2, grid=(n_blocks,),
            in_specs=[idx_spec,
                      pl.BlockSpec((bm, f), lambda i, be, nu: (i, 0)),
                      pl.BlockSpec((None, None, f, d), lambda i, be, nu: (layer, be[i], 0, 0))],
            out_specs=pl.BlockSpec(memory_space=pl.ANY),
            scratch_shapes=[pltpu.VMEM((2, bm, d), F32), pltpu.SemaphoreType.DMA((2,))]),
        out_shape=jax.ShapeDtypeStruct((n_out_rows, d), F32),
        compiler_params=_cparams("arbitrary"),
    )(block_expert, n_used, dst_row, hmid, w_down)


def _dispatch(route, counts, n_tok, n_experts):
    bm = MOE_ROWS
    n_assign = n_tok * TOP_K
    n_blocks = (n_assign + n_experts * (bm - 1)) // bm
    n_rows = n_blocks * bm
    cnt = counts[0, :n_experts].astype(jnp.int32)
    padded = (cnt + bm - 1) // bm * bm
    pad_ends = jnp.cumsum(padded)
    pad_starts = pad_ends - padded
    slot = jnp.take(pad_starts, route[:, :TOP_K], axis=0) + route[:, TOP_K:2 * TOP_K]
    code = jnp.arange(TOP_K, dtype=jnp.int32)[None, :] * n_tok + jnp.arange(n_tok, dtype=jnp.int32)[:, None]
    row_code = jnp.full((n_rows,), -1, jnp.int32).at[slot.reshape(-1)].set(code.reshape(-1), unique_indices=True)
    valid = row_code >= 0
    src_tok = jnp.where(valid, row_code % n_tok, 0)
    dst_row = jnp.where(valid, row_code, n_assign + jnp.arange(n_rows, dtype=jnp.int32))
    n_used = (pad_ends[-1] // bm).astype(jnp.int32)
    blk = jnp.minimum(jnp.arange(n_blocks, dtype=jnp.int32), n_used - 1)
    block_expert = jnp.minimum(jnp.searchsorted(pad_ends, blk * bm, side='right'), n_experts - 1).astype(jnp.int32)
    return (src_tok.reshape(n_blocks, 1, bm), dst_row.reshape(n_blocks, 1, bm), block_expert, n_used.reshape(1),
            n_assign + n_rows)


def _combine_kernel(x_ref, y0_ref, y1_ref, gate_ref, g2_ref, ng_ref, sh_ref, sc_ref, xo_ref, ho_ref, *, final):
    gate = gate_ref[...]
    f = gate[:, 0:1] * y0_ref[...] + gate[:, 1:2] * y1_ref[...]
    x = x_ref[...] + g2_ref[...] * f
    if final:
        xo_ref[...] = x * lax.rsqrt(jnp.mean(x * x, axis=-1, keepdims=True) + EPS) * ng_ref[...]
    else:
        xo_ref[...] = x
        ho_ref[...] = _rms_mod(x, ng_ref[...], sh_ref[...], sc_ref[...]).astype(BF16)


def _combine(dims, x1, y, gate, mod4, norm_g, layer, final):
    d = dims.d
    tm = ROW_TILE
    row = pl.BlockSpec((tm, d), lambda m: (m, 0))
    row1 = pl.BlockSpec((tm, d), lambda m: (dims.n_tok // tm + m, 0))
    nxt = layer if final else layer + 1
    ng = norm_g.reshape(-1, 1, d)
    ng_spec = pl.BlockSpec((None, 1, d), (lambda m: (0, 0, 0)) if final else (lambda m: (nxt, 0, 0)))
    in_specs = [row, row, row1, pl.BlockSpec((tm, LANES), lambda m: (m, 0)),
                _mod_spec(dims, layer, 5, tm), ng_spec,
                _mod_spec(dims, nxt, 0, tm), _mod_spec(dims, nxt, 1, tm)]
    out_shape = [jax.ShapeDtypeStruct((dims.n_tok, d), F32)]
    out_specs = [row]
    kern = functools.partial(_combine_kernel, final=final)
    if final:
        inner = kern
        kern = lambda *refs: inner(*refs, None)
    else:
        out_shape.append(jax.ShapeDtypeStruct((dims.n_tok, d), BF16))
        out_specs.append(row)
    return pl.pallas_call(
        kern, name="combine_final" if final else "combine",
        grid=(dims.n_tok // tm,),
        in_specs=in_specs, out_specs=out_specs, out_shape=out_shape,
        compiler_params=_cparams("parallel"),
    )(x1, y, y, gate, mod4, ng, mod4, mod4)


def _pair_gate_weights(lru_w):
    depth, n_dir, n_gate, nb, bw, _ = lru_w.shape
    assert n_dir == 2 and n_gate == 2 and 2 * bw == LANES and nb % 2 == 0
    w = lru_w.reshape(depth, 4, nb // 2, 2, bw, bw)
    eye = jnp.eye(2, dtype=lru_w.dtype)
    wp = jnp.einsum('lgpjio,jk->lpjigko', w, eye)
    return wp.reshape(depth, nb // 2, 2 * bw, 4 * 2 * bw)


def _rope_tables(seq, n_identity):
    rows = seq // GRID_W
    pos_row = jnp.repeat(jnp.arange(rows), GRID_W).astype(F32)
    pos_col = jnp.tile(jnp.arange(GRID_W), rows).astype(F32)
    inv = ROPE_BASE ** (-jnp.arange(ROPE_PAIRS, dtype=F32) / ROPE_PAIRS)
    ang_r = pos_row[:, None] * inv
    ang_c = pos_col[:, None] * inv
    ang = jnp.concatenate([ang_r, ang_r, ang_c, ang_c] * 2, axis=-1)
    cos = jnp.concatenate([jnp.cos(ang), jnp.ones((n_identity, LANES), F32)], axis=0)
    sin = jnp.concatenate([jnp.sin(ang), jnp.zeros((n_identity, LANES), F32)], axis=0)
    return cos, sin


def kernel(x, c, ctx, c_ctx, ada_w, ada_b, norm1_g, norm2_g, w_in, b_merge, rnn_conv_w, rnn_conv_b, lru_w, lru_b, lru_lam, diff_lam, diff_subln_g, sconv_w, w_branch, w_out, router_w, router_b, exp_w_gate, exp_w_up, exp_w_down, final_g):
    bsz, seq, d = x.shape
    ctx_len = ctx.shape[1]
    depth = ada_w.shape[0]
    width = rnn_conv_b.shape[1]
    n_experts = router_w.shape[1]
    d_in = w_in.shape[2]
    n_heads = (d_in - 5 * width - 3 * d) // (3 * V_DIM)
    assert n_heads * V_DIM == width and d % width == 0 and d_in == 8 * width + 3 * d
    dims = _Dims(bsz, seq, ctx_len, d)

    xt = jnp.concatenate([x.reshape(bsz * seq, d), ctx.reshape(bsz * ctx_len, d)], axis=0)
    cvec = jnp.zeros((SUBLANES, d), F32).at[:bsz].set(c).at[bsz].set(c_ctx)
    w_pair = _pair_gate_weights(lru_w)
    gate_b = lru_b.reshape(depth, 4, width)
    rw_pad = jnp.zeros((d, LANES), F32).at[:, :n_experts].set(router_w)
    rb_pad = jnp.zeros((1, LANES), F32).at[0, :n_experts].set(router_b)
    cos_t, sin_t = _rope_tables(seq, ROW_TILE)

    mod = _ada_modulation(cvec, ada_w, ada_b)
    mod4 = mod.reshape(depth, SUBLANES, 1, 6 * d)
    zeros_h0 = jnp.zeros((bsz, SUBLANES, width), F32)

    h = _norm_mod(dims, xt, norm1_g, mod4, 0)
    for layer in range(depth):
        last = layer == depth - 1
        lambda_init = 0.8 - 0.6 * math.exp(-0.3 * layer)
        u = _in_proj_f32(dims, h, w_in, layer, width, 2, 3)
        qkv = _in_proj_qkv(dims, h, w_in, cos_t, sin_t, layer, width, 2)
        coef = _lru_coefs(dims, u, rnn_conv_w, rnn_conv_b, w_pair, gate_b, lru_lam, layer, width)
        y_rnn, h_ctx = _lru_scan(dims, coef, u, zeros_h0, None, width, 1, True)
        y_rnn, _ = _lru_scan(dims, coef, u, h_ctx, y_rnn, width, 1, False)
        y_att = _diff_attn(dims, qkv, diff_lam, diff_subln_g, None, layer, n_heads, lambda_init, False)
        y_att = _diff_attn(dims, qkv, diff_lam, diff_subln_g, y_att, layer, n_heads, lambda_init, True)
        y_conv = _short_conv(dims, u, sconv_w, layer, width, 2)
        z = _merge(dims, y_rnn, y_att, y_conv, u, b_merge, w_branch, layer, width, 5 * width)
        x1, h2, route, gate, counts = _out_proj(dims, z, w_out, xt, norm2_g, mod4, rw_pad, rb_pad, layer, n_experts)
        src_tok, dst_row, block_expert, n_used, n_out_rows = _dispatch(route, counts, dims.n_tok, n_experts)
        y = _moe_experts(h2, src_tok, dst_row, block_expert, n_used, exp_w_gate, exp_w_up, exp_w_down, layer,
                         n_out_rows)
        if last:
            (xt,) = _combine(dims, x1, y, gate, mod4, final_g, layer, True)
        else:
            xt, h = _combine(dims, x1, y, gate, mod4, norm1_g, layer, False)
    return xt[:bsz * seq].reshape(bsz, seq, d)
```
